```python
import math
import jax
import jax.numpy as jnp
from jax import lax
import numpy as np

D_MODEL = 1024
BATCH = 8
SEQ = 8192
DEPTH = 1

N_MOD = 6
EPS = 1e-6
GM_CHUNK = 128
GM_HEAD = 128
GM_GROUPS = D_MODEL // GM_HEAD
GM_WIDTH = GM_GROUPS * GM_HEAD
SSM_INNER = 2 * D_MODEL
SSM_HEAD_DIM = 64
SSM_HEADS = SSM_INNER // SSM_HEAD_DIM
SSM_GROUPS = 8
SSM_STATE = 128
SSM_CONV = 4
SSM_CHUNK = 128
CONV_DIM = SSM_INNER + 2 * SSM_GROUPS * SSM_STATE
D_FF = 4 * D_MODEL
IN_SIZES = (GM_WIDTH, GM_WIDTH, SSM_INNER, CONV_DIM, SSM_HEADS, D_MODEL, D_MODEL)
IN_WIDTH = sum(IN_SIZES)

kernel_name = 'hybrid_sgu_ssd_block'


def _split_offsets(sizes):
    offs, acc = [], 0
    for s in sizes[:-1]:
        acc += s
        offs.append(acc)
    return offs


def rms_norm(x, w=None):
    xf = x.astype(jnp.float32)
    y = xf * lax.rsqrt(jnp.mean(jnp.square(xf), axis=-1, keepdims=True) + EPS)
    if w is not None:
        y = y * w.astype(jnp.float32)
    return y.astype(x.dtype)


def gated_group_rms_norm(y, z, w, groups):
    g = y.astype(jnp.float32) * jax.nn.silu(z.astype(jnp.float32))
    gs = g.reshape(*g.shape[:-1], groups, -1)
    gs = gs * lax.rsqrt(jnp.mean(jnp.square(gs), axis=-1, keepdims=True) + EPS)
    return (gs.reshape(g.shape) * w.astype(jnp.float32)).astype(y.dtype)


def sgu_branch(u, v, norm_w, ws, bs):
    b, s, _ = v.shape
    u = jax.nn.gelu(u)
    v = rms_norm(jax.nn.gelu(v), norm_w)
    vc = v.reshape(b, s // GM_CHUNK, GM_CHUNK, GM_GROUPS, GM_HEAD)
    causal = jnp.tril(jnp.ones((GM_CHUNK, GM_CHUNK), dtype=bool))
    ws_c = jnp.where(causal[None], ws, jnp.zeros_like(ws))
    sv = jnp.einsum('gij,bnjgd->bnigd', ws_c, vc) + bs.T[None, None, :, :, None]
    return u * sv.reshape(b, s, GM_WIDTH)


def ssd_scan(xs, dt, A, Bm, Cm):
    b, s, h, p = xs.shape
    g, n = Bm.shape[-2], Bm.shape[-1]
    hpg = h // g
    q = SSM_CHUNK
    nc = s // q

    def to_chunks(t):
        return jnp.swapaxes(t.reshape(b, nc, q, *t.shape[2:]), 0, 1)

    xdt = (xs.astype(jnp.float32) * dt[..., None]).reshape(b, s, g, hpg, p)
    a = (dt * A).reshape(b, s, g, hpg)
    causal = jnp.tril(jnp.ones((q, q), dtype=bool))

    def step(state, inp):
        x_c, a_c, B_c, C_c = inp
        cum = jnp.cumsum(a_c, axis=1)
        cum_t = jnp.moveaxis(cum, 1, -1)
        seg = cum_t[..., :, None] - cum_t[..., None, :]
        decay = jnp.exp(jnp.where(causal, seg, -jnp.inf))
        cb = jnp.einsum('bign,bjgn->bgij', C_c, B_c)
        y_intra = jnp.einsum('bgij,bghij,bjghp->bighp', cb, decay, x_c)
        y_inter = jnp.einsum('bign,bghpn->bighp', C_c, state) * jnp.exp(cum)[..., None]
        last = cum[:, -1]
        w_end = jnp.exp(last[:, None] - cum)
        new_state = state * jnp.exp(last)[..., None, None] + jnp.einsum(
            'bjgn,bjgh,bjghp->bghpn', B_c, w_end, x_c)
        return new_state, y_intra + y_inter

    state0 = jnp.zeros((b, g, hpg, p, n), jnp.float32)
    _, ys = lax.scan(step, state0, (to_chunks(xdt), to_chunks(a),
                                    to_chunks(Bm.astype(jnp.float32)),
                                    to_chunks(Cm.astype(jnp.float32))))
    return jnp.swapaxes(ys, 0, 1).reshape(b, s, h, p)


def ssd_branch(z, xbc, dt_raw, conv_w, conv_b, dt_bias, a_log, d_skip, norm_w):
    b, s, _ = xbc.shape
    xbc = lax.conv_general_dilated(
        xbc, conv_w[:, None, :], window_strides=(1,), padding=[(SSM_CONV - 1, 0)],
        dimension_numbers=('NWC', 'WIO', 'NWC'), feature_group_count=CONV_DIM) + conv_b
    xbc = jax.nn.silu(xbc)
    gn = SSM_GROUPS * SSM_STATE
    xs = xbc[..., :SSM_INNER].reshape(b, s, SSM_HEADS, SSM_HEAD_DIM)
    Bm = xbc[..., SSM_INNER:SSM_INNER + gn].reshape(b, s, SSM_GROUPS, SSM_STATE)
    Cm = xbc[..., SSM_INNER + gn:].reshape(b, s, SSM_GROUPS, SSM_STATE)
    dt = jax.nn.softplus(dt_raw.astype(jnp.float32) + dt_bias.astype(jnp.float32))
    A = -jnp.exp(a_log.astype(jnp.float32))
    y = ssd_scan(xs, dt, A, Bm, Cm)
    y = y + xs.astype(jnp.float32) * d_skip.astype(jnp.float32)[:, None]
    y = y.reshape(b, s, SSM_INNER).astype(z.dtype)
    return gated_group_rms_norm(y, z, norm_w, SSM_GROUPS)


def setup_inputs(seed: int = 0) -> dict:
    key = jax.random.key(seed)
    ks = jax.random.split(key, 22)
    f = jnp.float32
    L = DEPTH

    def nrm(k, shape, fan_in):
        return jax.random.normal(k, shape, f) * (fan_in ** -0.5)

    x = jax.random.normal(ks[0], (BATCH, SEQ, D_MODEL), f)
    c = jax.random.normal(ks[1], (BATCH, D_MODEL), f)
    w_mod = 0.5 * nrm(ks[2], (L, D_MODEL, N_MOD * D_MODEL), D_MODEL)
    b_mod = 0.01 * jax.random.normal(ks[3], (L, N_MOD * D_MODEL), f)
    w_in = nrm(ks[4], (L, D_MODEL, IN_WIDTH), D_MODEL)
    gm_norm_w = 1.0 + 0.05 * jax.random.normal(ks[5], (L, GM_WIDTH), f)
    gm_ws = nrm(ks[6], (L, GM_GROUPS, GM_CHUNK, GM_CHUNK), GM_CHUNK)
    gm_bs = 1.0 + 0.1 * jax.random.normal(ks[7], (L, GM_GROUPS, GM_CHUNK), f)
    conv_w = nrm(ks[8], (L, SSM_CONV, CONV_DIM), SSM_CONV)
    conv_b = 0.01 * jax.random.normal(ks[9], (L, CONV_DIM), f)
    dt0 = jnp.exp(jax.random.uniform(ks[10], (L, SSM_HEADS), f,
                                     minval=math.log(1e-3), maxval=math.log(1e-1)))
    dt_bias = dt0 + jnp.log(-jnp.expm1(-dt0))
    a_log = jnp.log(jax.random.uniform(ks[11], (L, SSM_HEADS), f, minval=1.0, maxval=16.0))
    d_skip = 1.0 + 0.1 * jax.random.normal(ks[12], (L, SSM_HEADS), f)
    ssm_norm_w = 1.0 + 0.05 * jax.random.normal(ks[13], (L, SSM_INNER), f)
    w_branch_gm = nrm(ks[14], (L, GM_WIDTH, D_MODEL), GM_WIDTH)
    w_branch_ssm = nrm(ks[15], (L, SSM_INNER, D_MODEL), SSM_INNER)
    w_out = nrm(ks[16], (L, D_MODEL, D_MODEL), D_MODEL)
    w_ff1 = nrm(ks[17], (L, D_MODEL, D_FF), D_MODEL)
    w_ff2 = nrm(ks[18], (L, D_FF, D_MODEL), D_FF)
    final_norm_w = 1.0 + 0.05 * jax.random.normal(ks[19], (D_MODEL,), f)
    return {'x': x, 'c': c, 'w_mod': w_mod, 'b_mod': b_mod, 'w_in': w_in,
            'gm_norm_w': gm_norm_w, 'gm_ws': gm_ws, 'gm_bs': gm_bs,
            'conv_w': conv_w, 'conv_b': conv_b, 'dt_bias': dt_bias, 'a_log': a_log,
            'd_skip': d_skip, 'ssm_norm_w': ssm_norm_w, 'w_branch_gm': w_branch_gm,
            'w_branch_ssm': w_branch_ssm, 'w_out': w_out, 'w_ff1': w_ff1,
            'w_ff2': w_ff2, 'final_norm_w': final_norm_w}


def reference(x, c, w_mod, b_mod, w_in, gm_norm_w, gm_ws, gm_bs, conv_w, conv_b,
              dt_bias, a_log, d_skip, ssm_norm_w, w_branch_gm, w_branch_ssm, w_out,
              w_ff1, w_ff2, final_norm_w):
    c_act = jax.nn.silu(c)
    offs = _split_offsets(IN_SIZES)
    for l in range(DEPTH):
        mod = (c_act @ w_mod[l] + b_mod[l])[:, None, :]
        sh1, sc1, g1, sh2, sc2, g2 = jnp.split(mod, N_MOD, axis=-1)
        h = rms_norm(x) * (1.0 + sc1) + sh1
        proj = h @ w_in[l]
        u, v, z, xbc, dt_raw, gate_a, gate_b = jnp.split(proj, offs, axis=-1)
        y_a = sgu_branch(u, v, gm_norm_w[l], gm_ws[l], gm_bs[l])
        y_b = ssd_branch(z, xbc, dt_raw, conv_w[l], conv_b[l], dt_bias[l], a_log[l],
                         d_skip[l], ssm_norm_w[l])
        mixed = (jax.nn.sigmoid(gate_a) * (y_a @ w_branch_gm[l])
                 + jax.nn.sigmoid(gate_b) * (y_b @ w_branch_ssm[l]))
        x = x + g1 * (mixed @ w_out[l])
        h2 = rms_norm(x) * (1.0 + sc2) + sh2
        x = x + g2 * (jnp.square(jax.nn.relu(h2 @ w_ff1[l])) @ w_ff2[l])
    return rms_norm(x, final_norm_w)
```

```python
import functools

import jax
import jax.numpy as jnp
from jax import lax
from jax.experimental import pallas as pl
from jax.experimental.pallas import tpu as pltpu

F32 = jnp.float32
BF16 = jnp.bfloat16

EPS = 1e-6
N_MOD = 6
CHUNK = 128
GM_HEAD = 128
SSM_HEAD_DIM = 64
SSM_GROUPS = 8
SSM_STATE = 128
SSM_CONV = 4
LANES = 128
SUBLANES = 8
VMEM_LIMIT_BYTES = 56 * 1024 * 1024


def _cparams(sem):
    return pltpu.CompilerParams(dimension_semantics=sem, vmem_limit_bytes=VMEM_LIMIT_BYTES)


def _rms(x):
    return x * lax.rsqrt(jnp.mean(x * x, axis=-1, keepdims=True) + EPS)


def _silu(x):
    return x * jax.nn.sigmoid(x)


def _softplus(x):
    return jnp.maximum(x, 0.0) + jnp.log1p(jnp.exp(-jnp.abs(x)))


def _mod_kernel(c_ref, w_ref, b_ref, o_ref):
    c = c_ref[...]
    o_ref[...] = jnp.dot(_silu(c), w_ref[...], preferred_element_type=F32,
                         precision=lax.Precision.HIGHEST) + b_ref[...]


def _mod(c, w_mod, b_mod):
    bsz, d = c.shape
    n = w_mod.shape[1]
    tn = d
    return pl.pallas_call(
        _mod_kernel,
        grid=(n // tn,),
        in_specs=[pl.BlockSpec((bsz, d), lambda j: (0, 0)),
                  pl.BlockSpec((d, tn), lambda j: (0, j)),
                  pl.BlockSpec((1, tn), lambda j: (0, j))],
        out_specs=pl.BlockSpec((bsz, tn), lambda j: (0, j)),
        out_shape=jax.ShapeDtypeStruct((bsz, n), F32),
        compiler_params=_cparams(("arbitrary",)),
        name="mod",
    )(c, w_mod, b_mod.reshape(1, n))


def _inproj_kernel(x_ref, mod_ref, w_ref, wdt_ref, out_ref, dt_ref, h_ref):
    @pl.when(pl.program_id(1) == 0)
    def _():
        h = _rms(x_ref[...]) * (1.0 + mod_ref[0, 1:2, :]) + mod_ref[0, 0:1, :]
        hb = h.astype(BF16)
        h_ref[...] = hb
        dt_ref[...] = jnp.dot(hb, wdt_ref[...], preferred_element_type=F32)

    out_ref[...] = jnp.dot(h_ref[...], w_ref[...], preferred_element_type=F32).astype(BF16)


def _in_proj(x2, mod3, w_main, w_dt, seq, tm, tn):
    t, d = x2.shape
    n = w_main.shape[1]
    tpb = seq // tm
    return pl.pallas_call(
        _inproj_kernel,
        grid=(t // tm, n // tn),
        in_specs=[pl.BlockSpec((tm, d), lambda i, j: (i, 0)),
                  pl.BlockSpec((1, N_MOD, d), lambda i, j: (i // tpb, 0, 0)),
                  pl.BlockSpec((d, tn), lambda i, j: (0, j)),
                  pl.BlockSpec((d, LANES), lambda i, j: (0, 0))],
        out_specs=[pl.BlockSpec((tm, tn), lambda i, j: (i, j)),
                   pl.BlockSpec((tm, LANES), lambda i, j: (i, 0))],
        out_shape=[jax.ShapeDtypeStruct((t, n), BF16),
                   jax.ShapeDtypeStruct((t, LANES), F32)],
        scratch_shapes=[pltpu.VMEM((tm, d), BF16)],
        compiler_params=_cparams(("arbitrary", "arbitrary")),
        name="in_proj",
    )(x2, mod3, w_main, w_dt)


def _sgu_kernel(u_ref, v_ref, nw_ref, ws_ref, bias_ref, ya_ref, vn_ref):
    tm, width = v_ref.shape
    v = jax.nn.gelu(v_ref[...].astype(F32))
    vn_ref[...] = (_rms(v) * nw_ref[...]).astype(BF16)
    row = lax.broadcasted_iota(jnp.int32, (CHUNK, CHUNK), 0)
    col = lax.broadcasted_iota(jnp.int32, (CHUNK, CHUNK), 1)
    causal = row >= col
    for g in range(width // GM_HEAD):
        cols = slice(g * GM_HEAD, (g + 1) * GM_HEAD)
        wg = jnp.where(causal, ws_ref[g], 0.0).astype(BF16)
        for c in range(tm // CHUNK):
            rows = slice(c * CHUNK, (c + 1) * CHUNK)
            sv = jnp.dot(wg, vn_ref[rows, cols], preferred_element_type=F32) + bias_ref[:, cols]
            u = jax.nn.gelu(u_ref[rows, cols].astype(F32))
            ya_ref[rows, cols] = (u * sv).astype(BF16)


def _sgu(proj, nw, ws, bias, width, tm):
    t = proj.shape[0]
    groups = width // GM_HEAD
    return pl.pallas_call(
        _sgu_kernel,
        grid=(t // tm,),
        in_specs=[pl.BlockSpec((tm, width), lambda i: (i, 0)),
                  pl.BlockSpec((tm, width), lambda i: (i, 1)),
                  pl.BlockSpec((1, width), lambda i: (0, 0)),
                  pl.BlockSpec((groups, CHUNK, CHUNK), lambda i: (0, 0, 0)),
                  pl.BlockSpec((CHUNK, width), lambda i: (0, 0))],
        out_specs=pl.BlockSpec((tm, width), lambda i: (i, 0)),
        out_shape=jax.ShapeDtypeStruct((t, width), BF16),
        scratch_shapes=[pltpu.VMEM((tm, width), BF16)],
        compiler_params=_cparams(("arbitrary",)),
        name="sgu",
    )(proj, proj, nw, ws, bias)


CONV_SLAB = 512


def _expand(v, e_ref):
    hi = v.astype(BF16)
    lo = (v - hi.astype(F32)).astype(BF16)
    return (jnp.dot(hi, e_ref[...], preferred_element_type=F32)
            + jnp.dot(lo, e_ref[...], preferred_element_type=F32))


def _ssd_kernel(xbc_ref, dt_ref, z_ref, cw_ref, cb_ref, dtb_ref, a_ref, dsk_ref, nw_ref, e_ref,
                y_ref, tail_ref, cbuf_ref, st_ref, xs_ref, b_ref, c_ref, yacc_ref):
    inner = xs_ref.shape[1]
    gn = b_ref.shape[1]
    conv_dim = inner + 2 * gn
    hpg = inner // SSM_HEAD_DIM // SSM_GROUPS
    gw = hpg * SSM_HEAD_DIM

    @pl.when(pl.program_id(1) == 0)
    def _():
        tail_ref[...] = jnp.zeros_like(tail_ref)
        st_ref[...] = jnp.zeros_like(st_ref)

    for s in range(0, conv_dim, CONV_SLAB):
        cs = slice(s, s + CONV_SLAB)
        cur = xbc_ref[:, cs].astype(F32)
        cbuf_ref[0:SUBLANES, :] = tail_ref[:, cs]
        cbuf_ref[SUBLANES:SUBLANES + CHUNK, :] = cur
        tail_ref[:, cs] = cur[CHUNK - SUBLANES:, :]
        acc = cb_ref[:, cs] + cw_ref[SSM_CONV - 1:SSM_CONV, cs] * cur
        for k in range(SSM_CONV - 1):
            off = SUBLANES - (SSM_CONV - 1) + k
            acc = acc + cw_ref[k:k + 1, cs] * cbuf_ref[off:off + CHUNK, :]
        act = _silu(acc)
        if s < inner:
            xs_ref[:, cs] = act
        elif s < inner + gn:
            b_ref[:, s - inner:s - inner + CONV_SLAB] = act
        else:
            c_ref[:, s - inner - gn:s - inner - gn + CONV_SLAB] = act.astype(BF16)

    row = lax.broadcasted_iota(jnp.int32, (CHUNK, CHUNK), 0)
    col = lax.broadcasted_iota(jnp.int32, (CHUNK, CHUNK), 1)
    causal = row >= col
    dt = _softplus(dt_ref[...] + dtb_ref[...])
    a = dt * -jnp.exp(a_ref[...])
    cum =jnp.dot(causal.astype(F32), a, preferred_element_type=F32, precision=lax.Precision.HIGHEST)
    cum_t = cum.T
    dt_t = dt.T
    last = cum[CHUNK - 1:CHUNK, :]
    ecum_x = _expand(jnp.exp(cum), e_ref)
    wend_x = _expand(jnp.exp(last - cum) * dt, e_ref)
    lane = lax.broadcasted_iota(jnp.int32, (CHUNK, LANES), 1)
    first_head = lane < SSM_HEAD_DIM

    for g in range(SSM_GROUPS):
        ns = slice(g * SSM_STATE, (g + 1) * SSM_STATE)
        xg = slice(g * gw, (g + 1) * gw)
        c_g = c_ref[:, ns]
        b_f = b_ref[:, ns]
        b_g = b_f.astype(BF16)
        cb = lax.dot_general(c_g, b_g, (((1,), (1,)), ((), ())), preferred_element_type=F32)
        st = st_ref[g]
        y_inter = jnp.dot(c_g, st.astype(BF16), preferred_element_type=F32) * ecum_x[:, xg]
        xw = (xs_ref[:, xg] * wend_x[:, xg]).astype(BF16)
        st_ref[g] = st * ecum_x[CHUNK - 1:CHUNK, xg] + jnp.dot(b_f.T.astype(BF16), xw,
                                                              preferred_element_type=F32)
        for pr in range(gw // LANES):
            ps = slice(g * gw + pr * LANES, g * gw + (pr + 1) * LANES)
            ms = []
            for hh in range(LANES // SSM_HEAD_DIM):
                h = (g * gw + pr * LANES) // SSM_HEAD_DIM + hh
                seg = cum[:, h:h + 1] - cum_t[h:h + 1, :]
                dec = jnp.exp(jnp.where(causal, seg, -jnp.inf))
                ms.append((cb * dec * dt_t[h:h + 1, :]).astype(BF16))
            xp = xs_ref[:, ps]
            xpb = xp.astype(BF16)
            rhs = jnp.concatenate([jnp.where(first_head, xpb, jnp.zeros_like(xpb)),
                                   jnp.where(first_head, jnp.zeros_like(xpb), xpb)], axis=0)
            y_intra = jnp.dot(jnp.concatenate(ms, axis=1), rhs, preferred_element_type=F32)
            yacc_ref[:, ps] = y_intra + y_inter[:, pr * LANES:(pr + 1) * LANES] + xp * dsk_ref[:, ps]

    ng = inner // SSM_GROUPS
    for g in range(SSM_GROUPS):
        cs = slice(g * ng, (g + 1) * ng)
        gg = yacc_ref[:, cs] * _silu(z_ref[:, cs].astype(F32))
        y_ref[:, cs] = (_rms(gg) * nw_ref[:, cs]).astype(BF16)


def _ssd(proj, dt_raw, cw, cb, dtb, a_neg, dsk, nw, e_exp, bsz, seq, inner, gn):
    t = proj.shape[0]
    conv_dim = inner + 2 * gn
    nc = seq // CHUNK
    gw = inner // SSM_GROUPS
    row = lambda b, c: (b * nc + c, 0)
    const = lambda b, c: (0, 0)
    return pl.pallas_call(
        _ssd_kernel,
        grid=(bsz, nc),
        in_specs=[pl.BlockSpec((CHUNK, conv_dim), lambda b, c: (b * nc + c, 1)),
                  pl.BlockSpec((CHUNK, LANES), row),
                  pl.BlockSpec((CHUNK, inner), lambda b, c: (b * nc + c, 1)),
                  pl.BlockSpec((SSM_CONV, conv_dim), const),
                  pl.BlockSpec((1, conv_dim), const),
                  pl.BlockSpec((1, LANES), const),
                  pl.BlockSpec((1, LANES), const),
                  pl.BlockSpec((1, inner), const),
                  pl.BlockSpec((1, inner), const),
                  pl.BlockSpec((LANES, inner), const)],
        out_specs=pl.BlockSpec((CHUNK, inner), row),
        out_shape=jax.ShapeDtypeStruct((t, inner), BF16),
        scratch_shapes=[pltpu.VMEM((SUBLANES, conv_dim), F32),
                        pltpu.VMEM((SUBLANES + CHUNK, CONV_SLAB), F32),
                        pltpu.VMEM((SSM_GROUPS, SSM_STATE, gw), F32),
                        pltpu.VMEM((CHUNK, inner), F32),
                        pltpu.VMEM((CHUNK, gn), F32),
                        pltpu.VMEM((CHUNK, gn), BF16),
                        pltpu.VMEM((CHUNK, inner), F32)],
        compiler_params=_cparams(("arbitrary", "arbitrary")),
        name="ssd",
    )(proj, dt_raw, proj, cw, cb, dtb, a_neg, dsk, nw, e_exp)


def _mix_kernel(ya_ref, yb_ref, ga_ref, gb_ref, x_ref, mod_ref, wgm_ref, wssm_ref, wout_ref, o_ref):
    pa = jnp.dot(ya_ref[...], wgm_ref[...], preferred_element_type=F32)
    pb = jnp.dot(yb_ref[...], wssm_ref[...], preferred_element_type=F32)
    mixed = (jax.nn.sigmoid(ga_ref[...].astype(F32)) * pa
             + jax.nn.sigmoid(gb_ref[...].astype(F32)) * pb)
    o = jnp.dot(mixed.astype(BF16), wout_ref[...], preferred_element_type=F32)
    o_ref[...] = x_ref[...] + mod_ref[0, 2:3, :] * o


def _mix(ya, yb, proj, x2, mod3, wgm, wssm, wout, seq, tm, ga_blk, gb_blk):
    t, d = x2.shape
    tpb = seq // tm
    whole = lambda i: (0, 0)
    return pl.pallas_call(
        _mix_kernel,
        grid=(t // tm,),
        in_specs=[pl.BlockSpec((tm, ya.shape[1]), lambda i: (i, 0)),
                  pl.BlockSpec((tm, yb.shape[1]), lambda i: (i, 0)),
                  pl.BlockSpec((tm, d), lambda i: (i, ga_blk)),
                  pl.BlockSpec((tm, d), lambda i: (i, gb_blk)),
                  pl.BlockSpec((tm, d), lambda i: (i, 0)),
                  pl.BlockSpec((1, N_MOD, d), lambda i: (i // tpb, 0, 0)),
                  pl.BlockSpec(wgm.shape, whole),
                  pl.BlockSpec(wssm.shape, whole),
                  pl.BlockSpec(wout.shape, whole)],
        out_specs=pl.BlockSpec((tm, d), lambda i: (i, 0)),
        out_shape=jax.ShapeDtypeStruct((t, d), F32),
        compiler_params=_cparams(("arbitrary",)),
        name="mix",
    )(ya, yb, proj, proj, x2, mod3, wgm, wssm, wout)


FF_SLAB = 1024


def _ffn_kernel(x_ref, mod_ref, w1_ref, w2_ref, fnw_ref, o_ref, *, final):
    x = x_ref[...]
    h = (_rms(x) * (1.0 + mod_ref[0, 4:5, :]) + mod_ref[0, 3:4, :]).astype(BF16)
    acc = jnp.zeros(x.shape, F32)
    for s in range(0, w1_ref.shape[1], FF_SLAB):
        hid = jnp.dot(h, w1_ref[:, s:s + FF_SLAB], preferred_element_type=F32)
        hid = jnp.square(jnp.maximum(hid, 0.0)).astype(BF16)
        acc = acc + jnp.dot(hid, w2_ref[s:s + FF_SLAB, :], preferred_element_type=F32)
    xo = x + mod_ref[0, 5:6, :] * acc
    if final:
        xo = _rms(xo) * fnw_ref[...]
    o_ref[...] = xo


def _ffn(x1, mod3, w1, w2, fnw, seq, tm, final):
    t, d = x1.shape
    tpb = seq // tm
    whole = lambda i: (0, 0)
    return pl.pallas_call(
        functools.partial(_ffn_kernel, final=final),
        grid=(t // tm,),
        in_specs=[pl.BlockSpec((tm, d), lambda i: (i, 0)),
                  pl.BlockSpec((1, N_MOD, d), lambda i: (i // tpb, 0, 0)),
                  pl.BlockSpec(w1.shape, whole),
                  pl.BlockSpec(w2.shape, whole),
                  pl.BlockSpec((1, d), whole)],
        out_specs=pl.BlockSpec((tm, d), lambda i: (i, 0)),
        out_shape=jax.ShapeDtypeStruct((t, d), F32),
        compiler_params=_cparams(("arbitrary",)),
        name="ffn",
    )(x1, mod3, w1, w2, fnw)


def kernel(x, c, w_mod, b_mod, w_in, gm_norm_w, gm_ws, gm_bs, conv_w, conv_b, dt_bias, a_log, d_skip,
           ssm_norm_w, w_branch_gm, w_branch_ssm, w_out, w_ff1, w_ff2, final_norm_w):
    bsz, seq, d = x.shape
    depth = w_mod.shape[0]
    t = bsz * seq
    gm_width = gm_norm_w.shape[1]
    inner = ssm_norm_w.shape[1]
    conv_dim = conv_w.shape[2]
    gn = (conv_dim - inner) // 2
    heads = dt_bias.shape[1]
    assert gm_width == d and inner == 2 * d and conv_dim == 4 * d and heads <= LANES
    assert seq % 1024 == 0 and heads * SSM_HEAD_DIM == inner and gn == SSM_GROUPS * SSM_STATE

    dt_off = 2 * gm_width + inner + conv_dim
    ga_blk = dt_off // d
    gb_blk = ga_blk + 1
    e_exp = (jnp.arange(LANES)[:, None] == (jnp.arange(inner) // SSM_HEAD_DIM)[None, :]).astype(BF16)

    x2 = x.reshape(t, d)
    for l in range(depth):
        w_main = jnp.concatenate([w_in[l][:, :dt_off], w_in[l][:, dt_off + heads:]], axis=1).astype(BF16)
        w_dt = jnp.pad(w_in[l][:, dt_off:dt_off + heads], ((0, 0), (0, LANES - heads))).astype(BF16)
        bias = jnp.broadcast_to(gm_bs[l].T[:, :, None], (CHUNK, gm_width // GM_HEAD, GM_HEAD)).reshape(CHUNK, gm_width)
        pad_h = lambda v: jnp.pad(v.reshape(1, heads), ((0, 0), (0, LANES - heads)))
        dsk = jnp.repeat(d_skip[l], SSM_HEAD_DIM).reshape(1, inner)

        mod3 = _mod(c, w_mod[l], b_mod[l]).reshape(bsz, N_MOD, d)
        proj, dt_raw = _in_proj(x2, mod3, w_main, w_dt, seq, tm=1024, tn=2048)
        ya = _sgu(proj, gm_norm_w[l].reshape(1, gm_width), gm_ws[l], bias, gm_width, tm=512)
        yb = _ssd(proj, dt_raw, conv_w[l], conv_b[l].reshape(1, conv_dim), pad_h(dt_bias[l]), pad_h(a_log[l]), dsk,
                  ssm_norm_w[l].reshape(1, inner), e_exp, bsz, seq, inner, gn)
        x1 = _mix(ya, yb, proj, x2, mod3, w_branch_gm[l].astype(BF16), w_branch_ssm[l].astype(BF16),
                  w_out[l].astype(BF16), seq, 512, ga_blk, gb_blk)
        x2 = _ffn(x1, mod3, w_ff1[l].astype(BF16), w_ff2[l].astype(BF16), final_norm_w.reshape(1, d),
                  seq, 512, final=(l == depth - 1))
    return x2.reshape(bsz, seq, d)
```
